```python
import math
import jax, jax.numpy as jnp
from jax import lax
import numpy as np

D_MODEL = 1024
BATCH = 8
SEQ = 4096
DEPTH = 1

GRID_W = 64
BLOCK = 128
WINDOW = 128
A_HEADS = 8
A_KV_HEADS = 2
A_HEAD_DIM = 64
B_HEADS = 4
B_KV_HEADS = 2
B_HEAD_DIM = 128
ROPE_THETA = 10000.0
D_FF = 2816
CONV_W = 3
LN_EPS = 1e-5
RMS_EPS = 1e-6
DEEPNORM_ALPHA = float((2 * DEPTH) ** 0.25)
DEEPNORM_BETA = float((8 * DEPTH) ** -0.25)

A_Q = A_HEADS * A_HEAD_DIM
A_KV = A_KV_HEADS * A_HEAD_DIM
B_Q = B_HEADS * B_HEAD_DIM
B_KV = B_KV_HEADS * B_HEAD_DIM
IN_WIDTHS = (A_Q, A_KV, A_KV, B_Q, B_KV, B_KV, D_MODEL, D_MODEL)
IN_SPLIT_POINTS = tuple(int(v) for v in np.cumsum(IN_WIDTHS)[:-1])
IN_TOTAL = int(sum(IN_WIDTHS))

kernel_name = "hybrid_gated_window_axial_gqa_convglu_deepnorm"


def layer_norm(x, g, b):
    xf = x.astype(jnp.float32)
    mu = xf.mean(-1, keepdims=True)
    var = jnp.square(xf - mu).mean(-1, keepdims=True)
    return ((xf - mu) * lax.rsqrt(var + LN_EPS) * g.astype(jnp.float32) + b.astype(jnp.float32)).astype(x.dtype)


def rms_norm(x, scale):
    xf = x.astype(jnp.float32)
    y = xf * lax.rsqrt(jnp.mean(xf * xf, axis=-1, keepdims=True) + RMS_EPS) * scale.astype(jnp.float32)
    return y.astype(x.dtype)


def alibi_slopes(n_heads):
    return jnp.exp2(-8.0 * jnp.arange(1, n_heads + 1, dtype=jnp.float32) / n_heads)


def rotate_axis(xs, pos):
    n = xs.shape[-1] // 2
    freqs = ROPE_THETA ** (-jnp.arange(n, dtype=jnp.float32) / n)
    ang = pos.astype(jnp.float32)[:, None] * freqs[None, :]
    cos = jnp.cos(ang)[None, :, None, :]
    sin = jnp.sin(ang)[None, :, None, :]
    x1 = xs[..., :n].astype(jnp.float32)
    x2 = xs[..., n:].astype(jnp.float32)
    return jnp.concatenate([x1 * cos - x2 * sin, x1 * sin + x2 * cos], axis=-1)


def axial_rope(x, rows, cols):
    half = x.shape[-1] // 2
    y = jnp.concatenate([rotate_axis(x[..., :half], rows), rotate_axis(x[..., half:], cols)], axis=-1)
    return y.astype(x.dtype)


def window_attention(q, k, v, sinks):
    bsz, s_len, hq, dh = q.shape
    hkv = k.shape[2]
    grp = hq // hkv
    nb = s_len // BLOCK
    qb = q.reshape(bsz, nb, BLOCK, hkv, grp, dh)
    pad = ((0, 0), (BLOCK, BLOCK), (0, 0), (0, 0))

    def band(t):
        tp = jnp.pad(t, pad).reshape(bsz, nb + 2, BLOCK, hkv, dh)
        return jnp.concatenate([tp[:, :-2], tp[:, 1:-1], tp[:, 2:]], axis=2)

    kb, vb = band(k), band(v)
    scores = jnp.einsum('bnqkgd,bnskd->bnkgqs', qb, kb).astype(jnp.float32) * (dh ** -0.5)

    blk = jnp.arange(nb)
    qpos = blk[:, None] * BLOCK + jnp.arange(BLOCK)[None, :]
    kpos = (blk[:, None] - 1) * BLOCK + jnp.arange(3 * BLOCK)[None, :]
    dist = jnp.abs(qpos[:, :, None] - kpos[:, None, :])
    valid = (dist <= WINDOW) & (kpos[:, None, :] >= 0) & (kpos[:, None, :] < s_len)
    slopes = alibi_slopes(hq).reshape(hkv, grp)
    bias = -slopes[None, :, :, None, None] * dist.astype(jnp.float32)[:, None, None, :, :]
    neg = jnp.finfo(jnp.float32).min
    scores = jnp.where(valid[:, None, None, :, :], scores + bias, neg)

    sink = sinks.astype(jnp.float32).reshape(hkv, grp)[:, :, None, None]
    m = jnp.maximum(scores.max(-1, keepdims=True), sink)
    p = jnp.exp(scores - m)
    p = p / (p.sum(-1, keepdims=True) + jnp.exp(sink - m))
    o = jnp.einsum('bnkgqs,bnskd->bnqkgd', p.astype(v.dtype), vb)
    return o.reshape(bsz, s_len, hq * dh)


def global_attention(q, k, v):
    bsz, s_len, hq, dh = q.shape
    hkv = k.shape[2]
    grp = hq // hkv
    nb = s_len // BLOCK
    qb = q.reshape(bsz, nb, BLOCK, hkv, grp, dh).transpose(1, 0, 2, 3, 4, 5)
    scale = dh ** -0.5

    def one_block(qi):
        s = jnp.einsum('bqkgd,bskd->bkgqs', qi, k).astype(jnp.float32) * scale
        p = jax.nn.softmax(s, axis=-1).astype(v.dtype)
        return jnp.einsum('bkgqs,bskd->bqkgd', p, v)

    o = lax.map(one_block, qb)
    return o.transpose(1, 0, 2, 3, 4, 5).reshape(bsz, s_len, hq * dh)


def token_mixer(h, w_in, b_in, a_sinks, b_q_norm, b_k_norm, w_o_a, w_o_b, w_out, rows, cols):
    bsz, s_len, _ = h.shape
    u = h @ w_in + b_in
    qa, ka, va, qb, kb, vb, ga, gb = jnp.split(u, IN_SPLIT_POINTS, axis=-1)
    qa = qa.reshape(bsz, s_len, A_HEADS, A_HEAD_DIM)
    ka = ka.reshape(bsz, s_len, A_KV_HEADS, A_HEAD_DIM)
    va = va.reshape(bsz, s_len, A_KV_HEADS, A_HEAD_DIM)
    oa = window_attention(qa, ka, va, a_sinks)
    qb = axial_rope(rms_norm(qb.reshape(bsz, s_len, B_HEADS, B_HEAD_DIM), b_q_norm), rows, cols)
    kb = axial_rope(rms_norm(kb.reshape(bsz, s_len, B_KV_HEADS, B_HEAD_DIM), b_k_norm), rows, cols)
    vb = vb.reshape(bsz, s_len, B_KV_HEADS, B_HEAD_DIM)
    ob = global_attention(qb, kb, vb)
    merged = jax.nn.sigmoid(ga) * (oa @ w_o_a) + jax.nn.sigmoid(gb) * (ob @ w_o_b)
    return merged @ w_out


def conv_glu(h, w_gate, w_val, conv_w, conv_b, w_down):
    g = h @ w_gate
    g = lax.conv_general_dilated(
        g, conv_w[:, None, :].astype(g.dtype), window_strides=(1,),
        padding=((CONV_W // 2, CONV_W // 2),),
        dimension_numbers=('NWC', 'WIO', 'NWC'),
        feature_group_count=g.shape[-1]) + conv_b
    act = jax.nn.gelu(g, approximate=False) * (h @ w_val)
    return act @ w_down


def setup_inputs(seed: int = 0) -> dict:
    key = jax.random.key(seed)
    ks = jax.random.split(key, 24)
    f32 = jnp.float32

    def nrm(k, shape, scale):
        return jax.random.normal(k, shape, f32) * scale

    x = jax.random.normal(ks[0], (BATCH, SEQ, D_MODEL), f32)
    ln_in_g = 1.0 + nrm(ks[1], (D_MODEL,), 0.02)
    ln_in_b = nrm(ks[2], (D_MODEL,), 0.02)
    col_scale = jnp.concatenate([
        jnp.full((w,), DEEPNORM_BETA if i in (2, 5) else 1.0, f32) for i, w in enumerate(IN_WIDTHS)])
    w_in = nrm(ks[3], (DEPTH, D_MODEL, IN_TOTAL), D_MODEL ** -0.5) * col_scale
    b_in = nrm(ks[4], (DEPTH, IN_TOTAL), 0.02)
    a_sinks = nrm(ks[5], (DEPTH, A_HEADS), 0.5)
    b_q_norm = 1.0 + nrm(ks[6], (DEPTH, B_HEAD_DIM), 0.02)
    b_k_norm = 1.0 + nrm(ks[7], (DEPTH, B_HEAD_DIM), 0.02)
    w_o_a = nrm(ks[8], (DEPTH, A_Q, D_MODEL), A_Q ** -0.5 * DEEPNORM_BETA)
    w_o_b = nrm(ks[9], (DEPTH, B_Q, D_MODEL), B_Q ** -0.5 * DEEPNORM_BETA)
    w_out = nrm(ks[10], (DEPTH, D_MODEL, D_MODEL), D_MODEL ** -0.5 * DEEPNORM_BETA)
    ln1_g = 1.0 + nrm(ks[11], (DEPTH, D_MODEL), 0.02)
    ln1_b = nrm(ks[12], (DEPTH, D_MODEL), 0.02)
    w_ffn_gate = nrm(ks[13], (DEPTH, D_MODEL, D_FF), D_MODEL ** -0.5 * DEEPNORM_BETA)
    w_ffn_val = nrm(ks[14], (DEPTH, D_MODEL, D_FF), D_MODEL ** -0.5 * DEEPNORM_BETA)
    ffn_conv_w = nrm(ks[15], (DEPTH, CONV_W, D_FF), CONV_W ** -0.5)
    ffn_conv_b = nrm(ks[16], (DEPTH, D_FF), 0.02)
    w_ffn_down = nrm(ks[17], (DEPTH, D_FF, D_MODEL), D_FF ** -0.5 * DEEPNORM_BETA)
    ln2_g = 1.0 + nrm(ks[18], (DEPTH, D_MODEL), 0.02)
    ln2_b = nrm(ks[19], (DEPTH, D_MODEL), 0.02)
    return {"x": x, "ln_in_g": ln_in_g, "ln_in_b": ln_in_b, "w_in": w_in, "b_in": b_in,
            "a_sinks": a_sinks, "b_q_norm": b_q_norm, "b_k_norm": b_k_norm,
            "w_o_a": w_o_a, "w_o_b": w_o_b, "w_out": w_out, "ln1_g": ln1_g, "ln1_b": ln1_b,
            "w_ffn_gate": w_ffn_gate, "w_ffn_val": w_ffn_val, "ffn_conv_w": ffn_conv_w,
            "ffn_conv_b": ffn_conv_b, "w_ffn_down": w_ffn_down, "ln2_g": ln2_g, "ln2_b": ln2_b}


def reference(x, ln_in_g, ln_in_b, w_in, b_in, a_sinks, b_q_norm, b_k_norm, w_o_a, w_o_b, w_out,
              ln1_g, ln1_b, w_ffn_gate, w_ffn_val, ffn_conv_w, ffn_conv_b, w_ffn_down, ln2_g, ln2_b):
    s_len = x.shape[1]
    n_rows = s_len // GRID_W
    rows = jnp.repeat(jnp.arange(n_rows), GRID_W)
    cols = jnp.tile(jnp.arange(GRID_W), n_rows)
    h = layer_norm(x, ln_in_g, ln_in_b)
    for l in range(DEPTH):
        y = token_mixer(h, w_in[l], b_in[l], a_sinks[l], b_q_norm[l], b_k_norm[l],
                        w_o_a[l], w_o_b[l], w_out[l], rows, cols)
        h = layer_norm(DEEPNORM_ALPHA * h + y, ln1_g[l], ln1_b[l])
        y = conv_glu(h, w_ffn_gate[l], w_ffn_val[l], ffn_conv_w[l], ffn_conv_b[l], w_ffn_down[l])
        h = layer_norm(DEEPNORM_ALPHA * h + y, ln2_g[l], ln2_b[l])
    return h
```

```python
import functools
import math

import jax
import jax.numpy as jnp
from jax import lax
from jax.experimental import pallas as pl
from jax.experimental.pallas import tpu as pltpu

F32 = jnp.float32
BF16 = jnp.bfloat16

D_MODEL = 1024
SEQ = 4096
DEPTH = 1
GRID_W = 64
BLOCK = 128
WINDOW = 128
A_HEADS, A_KV_HEADS, A_HEAD_DIM = 8, 2, 64
B_HEADS, B_KV_HEADS, B_HEAD_DIM = 4, 2, 128
ROPE_THETA = 10000.0
D_FF = 2816
LN_EPS = 1e-5
RMS_EPS = 1e-6
ALPHA = float((2 * DEPTH) ** 0.25)

A_Q = A_HEADS * A_HEAD_DIM
A_KV = A_KV_HEADS * A_HEAD_DIM
B_Q = B_HEADS * B_HEAD_DIM
B_KV = B_KV_HEADS * B_HEAD_DIM
OFF_QA, OFF_KA, OFF_VA = 0, A_Q, A_Q + A_KV
OFF_QB = A_Q + 2 * A_KV
OFF_KB, OFF_VB = OFF_QB + B_Q, OFF_QB + B_Q + B_KV
OFF_GA = OFF_QB + B_Q + 2 * B_KV
OFF_GB = OFF_GA + D_MODEL
IN_TOTAL = OFF_GB + D_MODEL

LANES = 128
SUBLANES = 8
VMEM_LIMIT = 56 * 1024 * 1024

TM_PROJ = 512
TM_FFN = 512
FF_CHUNK = 256
N_FF_CHUNKS = D_FF // FF_CHUNK
TQ_B = 256
TK_B = 512
NEG = float(jnp.finfo(jnp.float32).min)


def _layer_norm(x, g, b):
    mu = jnp.mean(x, axis=-1, keepdims=True)
    xc = x - mu
    var = jnp.mean(xc * xc, axis=-1, keepdims=True)
    return xc * lax.rsqrt(var + LN_EPS) * g + b


def _resident(shape):
    nd = len(shape)
    return pl.BlockSpec(shape, lambda *_: (0,) * nd, pipeline_mode=pl.Buffered(1))


def _rope_tables_kernel(freq_ref, cos_ref, sa_ref, sb_ref):
    rows = cos_ref.shape[0]
    t = pl.program_id(0) * rows + lax.broadcasted_iota(jnp.int32, (rows, LANES), 0)
    d = lax.broadcasted_iota(jnp.int32, (rows, LANES), 1)
    pos = jnp.where(d < B_HEAD_DIM // 2, t // GRID_W, t % GRID_W).astype(F32)
    ang = pos * freq_ref[...]
    c, s = jnp.cos(ang), jnp.sin(ang)
    first_half = (d % (B_HEAD_DIM // 2)) < B_HEAD_DIM // 4
    cos_ref[...] = c
    sa_ref[...] = jnp.where(first_half, -s, 0.0)
    sb_ref[...] = jnp.where(first_half, 0.0, s)


def _rope_tables(freq128):
    rows = 512
    out = jax.ShapeDtypeStruct((SEQ, LANES), F32)
    spec = pl.BlockSpec((rows, LANES), lambda i: (i, 0))
    return pl.pallas_call(
        _rope_tables_kernel,
        grid=(SEQ // rows,),
        in_specs=[pl.BlockSpec((1, LANES), lambda i: (0, 0))],
        out_specs=[spec, spec, spec],
        out_shape=[out, out, out],
        name="rope_tables",
    )(freq128)


def _pair_expand(kv):
    lane = lax.broadcasted_iota(jnp.int32, kv.shape, 1)
    lo = lane < A_HEAD_DIM
    sw = pltpu.roll(kv, A_HEAD_DIM, axis=1)
    zero = jnp.zeros_like(kv)
    return jnp.concatenate([jnp.where(lo, kv, zero), jnp.where(lo, zero, sw),
                            jnp.where(lo, sw, zero), jnp.where(lo, zero, kv)], axis=1)


def _rms_rope(x, scale, cos, sa, sb):
    y = x * lax.rsqrt(jnp.mean(x * x, axis=-1, keepdims=True) + RMS_EPS) * scale
    q = B_HEAD_DIM // 4
    return y * cos + pltpu.roll(y, B_HEAD_DIM - q, axis=1) * sa + pltpu.roll(y, q, axis=1) * sb


def _in_proj_kernel(x_ref, g_ref, b_ref, w_ref, bias_ref, qn_ref, kn_ref, cos_ref, sa_ref, sb_ref,
                    qa_ref, ka_ref, va_ref, qb_ref, kb_ref, vb_ref, sga_ref, sgb_ref):
    hb = _layer_norm(x_ref[...], g_ref[...], b_ref[...]).astype(BF16)

    def proj(lo, hi):
        return jnp.dot(hb, w_ref[:, lo:hi], preferred_element_type=F32) + bias_ref[:, lo:hi]

    ua = proj(OFF_QA, OFF_QB)
    qa_ref[...] = (ua[:, :A_Q] * (A_HEAD_DIM ** -0.5)).astype(BF16)
    ka_ref[...] = _pair_expand(ua[:, OFF_KA:OFF_VA]).astype(BF16)
    va_ref[...] = _pair_expand(ua[:, OFF_VA:OFF_QB]).astype(BF16)

    ub = proj(OFF_QB, OFF_GA)
    cos, sa, sb = cos_ref[...], sa_ref[...], sb_ref[...]
    for h in range(B_HEADS):
        sl = slice(h * B_HEAD_DIM, (h + 1) * B_HEAD_DIM)
        qb_ref[:, sl] = (_rms_rope(ub[:, sl], qn_ref[...], cos, sa, sb) * (B_HEAD_DIM ** -0.5)).astype(BF16)
    for h in range(B_KV_HEADS):
        sl = slice(h * B_HEAD_DIM, (h + 1) * B_HEAD_DIM)
        kb_ref[:, sl] = _rms_rope(ub[:, B_Q + h * B_HEAD_DIM:B_Q + (h + 1) * B_HEAD_DIM],
                                  kn_ref[...], cos, sa, sb).astype(BF16)
    vb_ref[...] = ub[:, B_Q + B_KV:].astype(BF16)

    sga_ref[...] = jax.nn.sigmoid(proj(OFF_GA, OFF_GB)).astype(BF16)
    sgb_ref[...] = jax.nn.sigmoid(proj(OFF_GB, IN_TOTAL)).astype(BF16)


def _in_proj(x2, ln_g, ln_b, w_in, b_in, qn, kn, cos, sa, sb):
    T = x2.shape[0]
    tm = TM_PROJ
    row = lambda n: pl.BlockSpec((tm, n), lambda i: (i, 0))
    tab = pl.BlockSpec((tm, LANES), lambda i: (i % (SEQ // tm), 0))
    widths = (A_Q, 4 * LANES, 4 * LANES, B_Q, B_KV, B_KV, D_MODEL, D_MODEL)
    return pl.pallas_call(
        _in_proj_kernel,
        grid=(T // tm,),
        in_specs=[row(D_MODEL), _resident((1, D_MODEL)), _resident((1, D_MODEL)),
                  _resident((D_MODEL, IN_TOTAL)), _resident((1, IN_TOTAL)),
                  _resident((1, B_HEAD_DIM)), _resident((1, B_HEAD_DIM)), tab, tab, tab],
        out_specs=[row(n) for n in widths],
        out_shape=[jax.ShapeDtypeStruct((T, n), BF16) for n in widths],
        compiler_params=pltpu.CompilerParams(dimension_semantics=("arbitrary",),
                                             vmem_limit_bytes=VMEM_LIMIT),
        name="in_proj",
    )(x2, ln_g, ln_b, w_in, b_in, qn, kn, cos, sa, sb)


def _window_attn_kernel(sink_ref, q_ref, kp_ref, kc_ref, kn_ref, vp_ref, vc_ref, vn_ref, o_ref):
    i = pl.program_id(1)
    nb = pl.num_programs(1)
    r = lax.broadcasted_iota(jnp.int32, (BLOCK, 3 * BLOCK), 0)
    c = lax.broadcasted_iota(jnp.int32, (BLOCK, 3 * BLOCK), 1)
    dist = jnp.abs(r + BLOCK - c)
    valid = ((dist <= WINDOW)
             & ((c >= BLOCK) | (i > 0))
             & ((c < 2 * BLOCK) | (i < nb - 1)))
    distf = dist.astype(F32)
    lane = lax.broadcasted_iota(jnp.int32, (BLOCK, LANES), 1)
    lo = lane < A_HEAD_DIM
    grp = A_HEADS // A_KV_HEADS
    for kv in range(A_KV_HEADS):
        def cat(p_ref, c_ref, n_ref):
            a = slice(2 * kv * LANES, (2 * kv + 1) * LANES)
            b = slice((2 * kv + 1) * LANES, (2 * kv + 2) * LANES)
            return jnp.concatenate([p_ref[:, a], c_ref[:, a], n_ref[:, a],
                                    p_ref[:, b], c_ref[:, b], n_ref[:, b]], axis=0)
        kcat = cat(kp_ref, kc_ref, kn_ref)
        vcat = cat(vp_ref, vc_ref, vn_ref)
        pairs = grp // 2
        q = jnp.concatenate([q_ref[:, (pairs * kv + j) * LANES:(pairs * kv + j + 1) * LANES]
                             for j in range(pairs)], axis=0)
        s_all = lax.dot_general(q, kcat, (((1,), (1,)), ((), ())), preferred_element_type=F32)
        p_rows, inv_l = [], []
        for j in range(pairs):
            p_cols, inv_j = [], []
            for e in range(2):
                head = kv * grp + 2 * j + e
                slope = 2.0 ** (-8.0 * (head + 1) / A_HEADS)
                sink = sink_ref[head]
                s = s_all[j * BLOCK:(j + 1) * BLOCK, e * 3 * BLOCK:(e + 1) * 3 * BLOCK]
                s = jnp.where(valid, s - slope * distf, NEG)
                m = jnp.maximum(jnp.max(s, axis=-1, keepdims=True), sink)
                p = jnp.exp(s - m)
                l = jnp.sum(p, axis=-1, keepdims=True) + jnp.exp(sink - m)
                p_cols.append(p.astype(BF16))
                inv_j.append(1.0 / l)
            p_rows.append(jnp.concatenate(p_cols, axis=1))
            inv_l.append(inv_j)
        p_all = jnp.concatenate(p_rows, axis=0)
        o_all = jnp.dot(p_all, vcat, preferred_element_type=F32)
        for j in range(pairs):
            scale = jnp.where(lo, inv_l[j][0], inv_l[j][1])
            col = (pairs * kv + j) * LANES
            o_ref[:, col:col + LANES] = (o_all[j * BLOCK:(j + 1) * BLOCK] * scale).astype(BF16)


def _window_attn(sinks, qa, ka, va, batch):
    T = qa.shape[0]
    nb = SEQ // BLOCK
    cur = lambda b, i: (b * nb + i, 0)
    prev = lambda b, i: (b * nb + jnp.maximum(i - 1, 0), 0)
    nxt = lambda b, i: (b * nb + jnp.minimum(i + 1, nb - 1), 0)
    blk = lambda fn: pl.BlockSpec((BLOCK, 4 * LANES), fn)
    return pl.pallas_call(
        _window_attn_kernel,
        grid=(batch, nb),
        in_specs=[pl.BlockSpec(memory_space=pltpu.SMEM),
                  pl.BlockSpec((BLOCK, A_Q), cur),
                  blk(prev), blk(cur), blk(nxt), blk(prev), blk(cur), blk(nxt)],
        out_specs=pl.BlockSpec((BLOCK, A_Q), cur),
        out_shape=jax.ShapeDtypeStruct((T, A_Q), BF16),
        compiler_params=pltpu.CompilerParams(dimension_semantics=("arbitrary", "arbitrary")),
        name="window_attn",
    )(sinks, qa, ka, ka, ka, va, va, va)


def _fold_lanes(x, op):
    acc = x[:, :LANES]
    for t in range(1, x.shape[1] // LANES):
        acc = op(acc, x[:, t * LANES:(t + 1) * LANES])
    return acc


def _global_attn_kernel(q_ref, k_ref, v_ref, o_ref, s_ref):
    grp = B_HEADS // B_KV_HEADS
    q = jnp.concatenate([q_ref[:, g * B_HEAD_DIM:(g + 1) * B_HEAD_DIM] for g in range(grp)], axis=0)
    n_chunks = SEQ // TK_B
    m_part = None
    for c in range(n_chunks):
        s = lax.dot_general(q, k_ref[c * TK_B:(c + 1) * TK_B, :], (((1,), (1,)), ((), ())),
                            preferred_element_type=F32)
        s_ref[:, c * TK_B:(c + 1) * TK_B] = s
        mc = _fold_lanes(s, jnp.maximum)
        m_part = mc if m_part is None else jnp.maximum(m_part, mc)
    m = jnp.max(m_part, axis=-1, keepdims=True)
    l_part = None
    o = None
    for c in range(n_chunks):
        p = jnp.exp(s_ref[:, c * TK_B:(c + 1) * TK_B] - m)
        lc = _fold_lanes(p, jnp.add)
        l_part = lc if l_part is None else l_part + lc
        oc = jnp.dot(p.astype(BF16), v_ref[c * TK_B:(c + 1) * TK_B, :], preferred_element_type=F32)
        o = oc if o is None else o + oc
    o = o * (1.0 / jnp.sum(l_part, axis=-1, keepdims=True))
    tq = q_ref.shape[0]
    for g in range(grp):
        o_ref[:, g * B_HEAD_DIM:(g + 1) * B_HEAD_DIM] = o[g * tq:(g + 1) * tq].astype(BF16)


def _global_attn(qb, kb, vb, batch):
    T = qb.shape[0]
    grp = B_HEADS // B_KV_HEADS
    nq = SEQ // TQ_B
    qspec = pl.BlockSpec((TQ_B, grp * B_HEAD_DIM), lambda b, h, i: (b * nq + i, h))
    kvspec = pl.BlockSpec((SEQ, B_HEAD_DIM), lambda b, h, i: (b, h))
    return pl.pallas_call(
        _global_attn_kernel,
        grid=(batch, B_KV_HEADS, nq),
        in_specs=[qspec, kvspec, kvspec],
        out_specs=qspec,
        out_shape=jax.ShapeDtypeStruct((T, B_Q), BF16),
        scratch_shapes=[pltpu.VMEM((grp * TQ_B, SEQ), F32)],
        compiler_params=pltpu.CompilerParams(
            dimension_semantics=("arbitrary", "arbitrary", "arbitrary"),
            vmem_limit_bytes=VMEM_LIMIT),
        name="global_attn",
    )(qb, kb, vb)


def _mix_out_kernel(x_ref, lg_ref, lb_ref, oa_ref, ob_ref, sga_ref, sgb_ref,
                    woa_ref, wob_ref, wout_ref, g1_ref, b1_ref, h1_ref):
    h0 = _layer_norm(x_ref[...], lg_ref[...], lb_ref[...])
    pa = jnp.dot(oa_ref[...], woa_ref[...], preferred_element_type=F32)
    pb = jnp.dot(ob_ref[...], wob_ref[...], preferred_element_type=F32)
    merged = sga_ref[...].astype(F32) * pa + sgb_ref[...].astype(F32) * pb
    y = jnp.dot(merged.astype(BF16), wout_ref[...], preferred_element_type=F32)
    h1_ref[...] = _layer_norm(ALPHA * h0 + y, g1_ref[...], b1_ref[...])


def _mix_out(x2, ln_g, ln_b, oa, ob, sga, sgb, woa, wob, wout, g1, b1):
    T = x2.shape[0]
    tm = TM_PROJ
    row = lambda n: pl.BlockSpec((tm, n), lambda i: (i, 0))
    vec = _resident((1, D_MODEL))
    return pl.pallas_call(
        _mix_out_kernel,
        grid=(T // tm,),
        in_specs=[row(D_MODEL), vec, vec, row(A_Q), row(B_Q), row(D_MODEL), row(D_MODEL),
                  _resident((A_Q, D_MODEL)), _resident((B_Q, D_MODEL)), _resident((D_MODEL, D_MODEL)),
                  vec, vec],
        out_specs=row(D_MODEL),
        out_shape=jax.ShapeDtypeStruct((T, D_MODEL), F32),
        compiler_params=pltpu.CompilerParams(dimension_semantics=("arbitrary",),
                                             vmem_limit_bytes=VMEM_LIMIT),
        name="mix_out",
    )(x2, ln_g, ln_b, oa, ob, sga, sgb, woa, wob, wout, g1, b1)


def _conv_glu_kernel(hp_ref, hc_ref, hn_ref, wg_ref, wv_ref, cw_ref, cb_ref, wd_ref, g2_ref, b2_ref,
                     out_ref, acc_ref):
    tm = hc_ref.shape[0]
    tiles_per_seq = SEQ // tm
    pos = pl.program_id(0) % tiles_per_seq
    hp = jnp.where(pos == 0, 0.0, hp_ref[...])
    hn = jnp.where(pos == tiles_per_seq - 1, 0.0, hn_ref[...])
    hc = hc_ref[...]
    hcb = hc.astype(BF16)
    hext = jnp.concatenate([hp, hc, hn], axis=0).astype(BF16)
    acc_ref[...] = jnp.zeros_like(acc_ref)

    def body(c, carry):
        g = jnp.dot(hext, wg_ref[c], preferred_element_type=F32)
        rows = g.shape[0]
        g_prev = pltpu.roll(g, 1, axis=0)[SUBLANES:SUBLANES + tm]
        g_next = pltpu.roll(g, rows - 1, axis=0)[SUBLANES:SUBLANES + tm]
        cw = cw_ref[c]
        conv = g_prev * cw[0:1] + g[SUBLANES:SUBLANES + tm] * cw[1:2] + g_next * cw[2:3] + cb_ref[c]
        gelu = 0.5 * conv * (1.0 + lax.erf(conv * (1.0 / math.sqrt(2.0))))
        val = jnp.dot(hcb, wv_ref[c], preferred_element_type=F32)
        act = (gelu * val).astype(BF16)
        acc_ref[...] += jnp.dot(act, wd_ref[c], preferred_element_type=F32)
        return carry

    lax.fori_loop(0, N_FF_CHUNKS, body, 0)
    out_ref[...] = _layer_norm(ALPHA * hc + acc_ref[...], g2_ref[...], b2_ref[...])


def _conv_glu(h1, wg, wv, cw, cb, wd, g2, b2):
    T = h1.shape[0]
    tm = TM_FFN
    per = tm // SUBLANES
    last = T // SUBLANES - 1
    halo = lambda fn: pl.BlockSpec((SUBLANES, D_MODEL), fn)
    return pl.pallas_call(
        _conv_glu_kernel,
        grid=(T // tm,),
        in_specs=[halo(lambda i: (jnp.maximum(i * per - 1, 0), 0)),
                  pl.BlockSpec((tm, D_MODEL), lambda i: (i, 0)),
                  halo(lambda i: (jnp.minimum((i + 1) * per, last), 0)),
                  _resident((N_FF_CHUNKS, D_MODEL, FF_CHUNK)), _resident((N_FF_CHUNKS, D_MODEL, FF_CHUNK)),
                  _resident((N_FF_CHUNKS, 3, FF_CHUNK)), _resident((N_FF_CHUNKS, 1, FF_CHUNK)),
                  _resident((N_FF_CHUNKS, FF_CHUNK, D_MODEL)),
                  _resident((1, D_MODEL)), _resident((1, D_MODEL))],
        out_specs=pl.BlockSpec((tm, D_MODEL), lambda i: (i, 0)),
        out_shape=jax.ShapeDtypeStruct((T, D_MODEL), F32),
        scratch_shapes=[pltpu.VMEM((tm, D_MODEL), F32)],
        compiler_params=pltpu.CompilerParams(dimension_semantics=("arbitrary",),
                                             vmem_limit_bytes=VMEM_LIMIT),
        name="conv_glu",
    )(h1, h1, h1, wg, wv, cw, cb, wd, g2, b2)


def _chunk_cols(w):
    return w.reshape(w.shape[0], N_FF_CHUNKS, FF_CHUNK).transpose(1, 0, 2)


def kernel(x, ln_in_g, ln_in_b, w_in, b_in, a_sinks, b_q_norm, b_k_norm, w_o_a, w_o_b, w_out,
           ln1_g, ln1_b, w_ffn_gate, w_ffn_val, ffn_conv_w, ffn_conv_b, w_ffn_down, ln2_g, ln2_b):
    batch, s_len, d = x.shape
    assert (s_len, d) == (SEQ, D_MODEL) and w_in.shape[0] == DEPTH
    x2 = x.reshape(batch * s_len, d)
    vec = lambda v: v.reshape(1, -1).astype(F32)

    n = B_HEAD_DIM // 4
    freqs = ROPE_THETA ** (-jnp.arange(n, dtype=F32) / n)
    cos, sa, sb = _rope_tables(jnp.tile(freqs, 4).reshape(1, LANES))

    qa, ka, va, qb, kb, vb, sga, sgb = _in_proj(
        x2, vec(ln_in_g), vec(ln_in_b), w_in[0].astype(BF16), vec(b_in[0]),
        vec(b_q_norm[0]), vec(b_k_norm[0]), cos, sa, sb)
    oa = _window_attn(a_sinks[0].astype(F32), qa, ka, va, batch)
    ob = _global_attn(qb, kb, vb, batch)
    h1 = _mix_out(x2, vec(ln_in_g), vec(ln_in_b), oa, ob, sga, sgb,
                  w_o_a[0].astype(BF16), w_o_b[0].astype(BF16), w_out[0].astype(BF16),
                  vec(ln1_g[0]), vec(ln1_b[0]))
    out = _conv_glu(h1, _chunk_cols(w_ffn_gate[0].astype(BF16)), _chunk_cols(w_ffn_val[0].astype(BF16)),
                    ffn_conv_w[0].reshape(3, N_FF_CHUNKS, FF_CHUNK).transpose(1, 0, 2),
                    ffn_conv_b[0].reshape(N_FF_CHUNKS, 1, FF_CHUNK),
                    w_ffn_down[0].astype(BF16).reshape(N_FF_CHUNKS, FF_CHUNK, D_MODEL),
                    vec(ln2_g[0]), vec(ln2_b[0]))
    return out.reshape(batch, s_len, d)
```
